```python
import math
import jax, jax.numpy as jnp
from jax import lax
import numpy as np

D_MODEL = 4096
BATCH = 4
SEQ = 2048
DEPTH = 2

HEAD_DIM = 128
ROPE_THETA = 10000.0
RMS_EPS = 1e-6
D_FF = 11008
Q_BLOCK = 128
NEG = -1e30
BIG = 1e9

FOX_HEADS = 16
DIFF_HEADS = 8
DIFF_V_DIM = 2 * HEAD_DIM
IN0_WIDTH = 3 * FOX_HEADS * HEAD_DIM + FOX_HEADS + 2 * DIFF_HEADS * 2 * HEAD_DIM + DIFF_HEADS * DIFF_V_DIM
MIX0_OUT = FOX_HEADS * HEAD_DIM + DIFF_HEADS * DIFF_V_DIM

NSA_HEADS = 32
NSA_KV_HEADS = 4
NSA_GROUP = NSA_HEADS // NSA_KV_HEADS
CMP_BLOCK = 32
CMP_STRIDE = 16
CMP_HIDDEN = 256
SEL_BLOCK = 64
N_SELECT = 16
WINDOW = 512
SEL_Q_CHUNK = 64
N_BRANCH = 3
IN1_WIDTH = NSA_HEADS * HEAD_DIM + 6 * NSA_KV_HEADS * HEAD_DIM + N_BRANCH * NSA_HEADS
MIX1_OUT = NSA_HEADS * HEAD_DIM

N_EVEN = (DEPTH + 1) // 2
N_ODD = DEPTH // 2

kernel_name = "hybrid_fox_diff_nsa_macaron"


def rms_norm(x, g):
    xf = x.astype(jnp.float32)
    y = xf * lax.rsqrt(jnp.mean(xf * xf, axis=-1, keepdims=True) + RMS_EPS)
    return (y * g.astype(jnp.float32)).astype(x.dtype)


def swiglu(x, w_gate, w_up, w_down):
    return (jax.nn.silu(x @ w_gate) * (x @ w_up)) @ w_down


def rope(x, pos):
    half = x.shape[-1] // 2
    inv = ROPE_THETA ** (-jnp.arange(half, dtype=jnp.float32) / half)
    ang = pos.astype(jnp.float32)[:, None] * inv[None, :]
    shape = (1, x.shape[1]) + (1,) * (x.ndim - 3) + (half,)
    cos = jnp.cos(ang).reshape(shape)
    sin = jnp.sin(ang).reshape(shape)
    x1 = x[..., :half].astype(jnp.float32)
    x2 = x[..., half:].astype(jnp.float32)
    return jnp.concatenate([x1 * cos - x2 * sin, x2 * cos + x1 * sin], axis=-1).astype(x.dtype)


def split_cols(proj, sizes):
    return jnp.split(proj, np.cumsum(sizes)[:-1].tolist(), axis=-1)


def fox_diff_mixer(h, w_in, b_forget, lq1, lk1, lq2, lk2, subln_g, w_out, lam_init):
    B, T, _ = h.shape
    proj = h @ w_in
    fq, fk, fv, fgate, dq, dk, dv = split_cols(
        proj, [FOX_HEADS * HEAD_DIM] * 3 + [FOX_HEADS] + [DIFF_HEADS * 2 * HEAD_DIM] * 2 + [DIFF_HEADS * DIFF_V_DIM])
    pos = jnp.arange(T)
    fq = fq.reshape(B, T, FOX_HEADS, HEAD_DIM)
    fk = fk.reshape(B, T, FOX_HEADS, HEAD_DIM)
    fv = fv.reshape(B, T, FOX_HEADS, HEAD_DIM)
    log_f = jax.nn.log_sigmoid(fgate.astype(jnp.float32) + b_forget.astype(jnp.float32))
    cum = jnp.transpose(jnp.cumsum(log_f, axis=1), (0, 2, 1))
    dq = rope(dq.reshape(B, T, DIFF_HEADS, 2, HEAD_DIM), pos)
    dk = rope(dk.reshape(B, T, DIFF_HEADS, 2, HEAD_DIM), pos)
    dv = dv.reshape(B, T, DIFF_HEADS, DIFF_V_DIM)
    lam = (jnp.exp(jnp.sum(lq1.astype(jnp.float32) * lk1.astype(jnp.float32)))
           - jnp.exp(jnp.sum(lq2.astype(jnp.float32) * lk2.astype(jnp.float32))) + lam_init)
    scale = HEAD_DIM ** -0.5

    def block(i):
        start = i * Q_BLOCK
        tq = start + jnp.arange(Q_BLOCK)
        causal = tq[:, None] >= pos[None, :]
        q = lax.dynamic_slice_in_dim(fq, start, Q_BLOCK, axis=1)
        cq = lax.dynamic_slice_in_dim(cum, start, Q_BLOCK, axis=2)
        s = jnp.einsum('bqhd,bshd->bhqs', q, fk).astype(jnp.float32) * scale
        s = s + (cq[..., :, None] - cum[..., None, :])
        p = jax.nn.softmax(jnp.where(causal, s, NEG), axis=-1)
        o_fox = jnp.einsum('bhqs,bshd->bqhd', p.astype(fv.dtype), fv)
        q2 = lax.dynamic_slice_in_dim(dq, start, Q_BLOCK, axis=1)
        s2 = jnp.einsum('bqhcd,bshcd->bhcqs', q2, dk).astype(jnp.float32) * scale
        p2 = jax.nn.softmax(jnp.where(causal, s2, NEG), axis=-1)
        a = p2[:, :, 0] - lam * p2[:, :, 1]
        o_diff = jnp.einsum('bhqs,bshe->bqhe', a.astype(dv.dtype), dv)
        return o_fox, o_diff

    o_fox, o_diff = lax.map(block, jnp.arange(T // Q_BLOCK))
    o_fox = jnp.moveaxis(o_fox, 0, 1).reshape(B, T, FOX_HEADS * HEAD_DIM)
    o_diff = jnp.moveaxis(o_diff, 0, 1).reshape(B, T, DIFF_HEADS, DIFF_V_DIM)
    o_diff = rms_norm(o_diff, subln_g) * (1.0 - lam_init)
    out = jnp.concatenate([o_fox, o_diff.reshape(B, T, DIFF_HEADS * DIFF_V_DIM)], axis=-1)
    return out @ w_out


def nsa_mixer(h, w_in, pe_k, k_w1, k_b1, k_w2, pe_v, v_w1, v_b1, v_w2, w_out):
    B, T, _ = h.shape
    G, R, d = NSA_KV_HEADS, NSA_GROUP, HEAD_DIM
    proj = h @ w_in
    q, kc, vc, ks, vs, kw, vw, gates = split_cols(proj, [NSA_HEADS * d] + [G * d] * 6 + [N_BRANCH * NSA_HEADS])
    q = q.reshape(B, T, G, R, d)
    kc, vc, ks, vs, kw, vw = [a.reshape(B, T, G, d) for a in (kc, vc, ks, vs, kw, vw)]
    gates = jax.nn.sigmoid(gates.astype(jnp.float32)).reshape(B, T, G, R, N_BRANCH).astype(h.dtype)
    pos = jnp.arange(T)
    q_rot = rope(q, pos)
    ks = rope(ks, pos)
    kw = rope(kw, pos)
    scale = d ** -0.5

    n_cmp = (T - CMP_BLOCK) // CMP_STRIDE + 1
    cmp_idx = np.arange(n_cmp)[:, None] * CMP_STRIDE + np.arange(CMP_BLOCK)[None, :]

    def compress(a, pe, w1, b1, w2):
        blk = a[:, cmp_idx] + pe[:, None, :]
        blk = jnp.moveaxis(blk, 3, 2).reshape(B, n_cmp, G, CMP_BLOCK * d)
        return jax.nn.silu(blk @ w1 + b1) @ w2

    k_cmp = compress(kc, pe_k, k_w1, k_b1, k_w2)
    v_cmp = compress(vc, pe_v, v_w1, v_b1, v_w2)
    cmp_mask = pos[:, None] >= jnp.asarray(cmp_idx[:, -1])[None, :]
    s = jnp.einsum('btgrd,bngd->bgrtn', q, k_cmp).astype(jnp.float32) * scale
    p_cmp = jax.nn.softmax(jnp.where(cmp_mask, s, NEG), axis=-1) * cmp_mask
    o_cmp = jnp.einsum('bgrtn,bngd->btgrd', p_cmp.astype(v_cmp.dtype), v_cmp)

    n_sel = T // SEL_BLOCK
    n_top = min(N_SELECT, n_sel)
    cmp_start = cmp_idx[:, 0]
    sel_start = np.arange(n_sel) * SEL_BLOCK
    overlap = ((cmp_start[:, None] < sel_start[None, :] + SEL_BLOCK)
               & (cmp_start[:, None] + CMP_BLOCK > sel_start[None, :])).astype(np.float32)
    imp = jnp.einsum('bgrtn,nj->bgtj', p_cmp, jnp.asarray(overlap))
    blk_t = pos // SEL_BLOCK
    j = jnp.arange(n_sel)
    valid = j[None, :] <= blk_t[:, None]
    forced = (j[None, :] == 0) | (j[None, :] == blk_t[:, None]) | (j[None, :] == blk_t[:, None] - 1)
    score = jnp.where(valid, jnp.where(forced, BIG, imp), -BIG)
    top_val, top_idx = lax.top_k(score, n_top)
    top_ok = top_val > -BIG / 2
    ks_blk = jnp.transpose(ks.reshape(B, n_sel, SEL_BLOCK, G, d), (0, 3, 1, 2, 4))
    vs_blk = jnp.transpose(vs.reshape(B, n_sel, SEL_BLOCK, G, d), (0, 3, 1, 2, 4))
    bi = jnp.arange(B)[:, None, None, None]
    gi = jnp.arange(G)[None, :, None, None]

    def sel_chunk(i):
        start = i * SEL_Q_CHUNK
        tq = start + jnp.arange(SEL_Q_CHUNK)
        qc = lax.dynamic_slice_in_dim(q_rot, start, SEL_Q_CHUNK, axis=1)
        idx = lax.dynamic_slice_in_dim(top_idx, start, SEL_Q_CHUNK, axis=2)
        ok = lax.dynamic_slice_in_dim(top_ok, start, SEL_Q_CHUNK, axis=2)
        kg = ks_blk[bi, gi, idx]
        vg = vs_blk[bi, gi, idx]
        key_pos = idx[..., None] * SEL_BLOCK + jnp.arange(SEL_BLOCK)
        mask = ok[..., None] & (key_pos <= tq[None, None, :, None, None])
        s = jnp.einsum('bqgrd,bgqkld->bgrqkl', qc, kg).astype(jnp.float32) * scale
        s = jnp.where(mask[:, :, None], s, NEG).reshape(B, G, R, SEL_Q_CHUNK, n_top * SEL_BLOCK)
        p = jax.nn.softmax(s, axis=-1).reshape(B, G, R, SEL_Q_CHUNK, n_top, SEL_BLOCK)
        return jnp.einsum('bgrqkl,bgqkld->bqgrd', p.astype(vg.dtype), vg)

    o_sel = lax.map(sel_chunk, jnp.arange(T // SEL_Q_CHUNK))
    o_sel = jnp.moveaxis(o_sel, 0, 1).reshape(B, T, G, R, d)

    kw_pad = jnp.pad(kw, ((0, 0), (WINDOW, 0), (0, 0), (0, 0)))
    vw_pad = jnp.pad(vw, ((0, 0), (WINDOW, 0), (0, 0), (0, 0)))
    span = WINDOW + Q_BLOCK

    def win_block(i):
        start = i * Q_BLOCK
        tq = start + jnp.arange(Q_BLOCK)
        kpos = start - WINDOW + jnp.arange(span)
        qb = lax.dynamic_slice_in_dim(q_rot, start, Q_BLOCK, axis=1)
        kb = lax.dynamic_slice_in_dim(kw_pad, start, span, axis=1)
        vb = lax.dynamic_slice_in_dim(vw_pad, start, span, axis=1)
        dist = tq[:, None] - kpos[None, :]
        mask = (kpos[None, :] >= 0) & (dist >= 0) & (dist < WINDOW)
        s = jnp.einsum('bqgrd,bsgd->bgrqs', qb, kb).astype(jnp.float32) * scale
        p = jax.nn.softmax(jnp.where(mask, s, NEG), axis=-1)
        return jnp.einsum('bgrqs,bsgd->bqgrd', p.astype(vb.dtype), vb)

    o_win = lax.map(win_block, jnp.arange(T // Q_BLOCK))
    o_win = jnp.moveaxis(o_win, 0, 1).reshape(B, T, G, R, d)

    out = gates[..., 0:1] * o_cmp + gates[..., 1:2] * o_sel + gates[..., 2:3] * o_win
    return out.reshape(B, T, MIX1_OUT) @ w_out


def setup_inputs(seed: int = 0) -> dict:
    key = jax.random.key(seed)
    keys = iter(jax.random.split(key, 32))

    def nrm(shape, scale):
        return jax.random.normal(next(keys), shape, jnp.float32) * scale

    def gain(shape):
        return 1.0 + 0.02 * jax.random.normal(next(keys), shape, jnp.float32)

    D = D_MODEL
    return {
        "x": nrm((BATCH, SEQ, D), 1.0),
        "ffn1_norm": gain((DEPTH, D)),
        "ffn1_w_gate": nrm((DEPTH, D, D_FF), D ** -0.5),
        "ffn1_w_up": nrm((DEPTH, D, D_FF), D ** -0.5),
        "ffn1_w_down": nrm((DEPTH, D_FF, D), D_FF ** -0.5),
        "mix_norm": gain((DEPTH, D)),
        "ffn2_norm": gain((DEPTH, D)),
        "ffn2_w_gate": nrm((DEPTH, D, D_FF), D ** -0.5),
        "ffn2_w_up": nrm((DEPTH, D, D_FF), D ** -0.5),
        "ffn2_w_down": nrm((DEPTH, D_FF, D), D_FF ** -0.5),
        "even_w_in": nrm((N_EVEN, D, IN0_WIDTH), D ** -0.5),
        "even_b_forget": nrm((N_EVEN, FOX_HEADS), 0.1),
        "even_lambda_q1": nrm((N_EVEN, HEAD_DIM), 0.1),
        "even_lambda_k1": nrm((N_EVEN, HEAD_DIM), 0.1),
        "even_lambda_q2": nrm((N_EVEN, HEAD_DIM), 0.1),
        "even_lambda_k2": nrm((N_EVEN, HEAD_DIM), 0.1),
        "even_subln": gain((N_EVEN, DIFF_V_DIM)),
        "even_w_out": nrm((N_EVEN, MIX0_OUT, D), MIX0_OUT ** -0.5),
        "odd_w_in": nrm((N_ODD, D, IN1_WIDTH), D ** -0.5),
        "odd_cmp_pe_k": nrm((N_ODD, CMP_BLOCK, HEAD_DIM), 0.02),
        "odd_cmp_k_w1": nrm((N_ODD, CMP_BLOCK * HEAD_DIM, CMP_HIDDEN), (CMP_BLOCK * HEAD_DIM) ** -0.5),
        "odd_cmp_k_b1": nrm((N_ODD, CMP_HIDDEN), 0.01),
        "odd_cmp_k_w2": nrm((N_ODD, CMP_HIDDEN, HEAD_DIM), CMP_HIDDEN ** -0.5),
        "odd_cmp_pe_v": nrm((N_ODD, CMP_BLOCK, HEAD_DIM), 0.02),
        "odd_cmp_v_w1": nrm((N_ODD, CMP_BLOCK * HEAD_DIM, CMP_HIDDEN), (CMP_BLOCK * HEAD_DIM) ** -0.5),
        "odd_cmp_v_b1": nrm((N_ODD, CMP_HIDDEN), 0.01),
        "odd_cmp_v_w2": nrm((N_ODD, CMP_HIDDEN, HEAD_DIM), CMP_HIDDEN ** -0.5),
        "odd_w_out": nrm((N_ODD, MIX1_OUT, D), MIX1_OUT ** -0.5),
        "final_norm": gain((D,)),
    }


def reference(x, ffn1_norm, ffn1_w_gate, ffn1_w_up, ffn1_w_down, mix_norm,
              ffn2_norm, ffn2_w_gate, ffn2_w_up, ffn2_w_down,
              even_w_in, even_b_forget, even_lambda_q1, even_lambda_k1, even_lambda_q2, even_lambda_k2,
              even_subln, even_w_out,
              odd_w_in, odd_cmp_pe_k, odd_cmp_k_w1, odd_cmp_k_b1, odd_cmp_k_w2,
              odd_cmp_pe_v, odd_cmp_v_w1, odd_cmp_v_b1, odd_cmp_v_w2, odd_w_out,
              final_norm):
    h = x
    for layer in range(DEPTH):
        h = h + 0.5 * swiglu(rms_norm(h, ffn1_norm[layer]), ffn1_w_gate[layer], ffn1_w_up[layer], ffn1_w_down[layer])
        hn = rms_norm(h, mix_norm[layer])
        i = layer // 2
        if layer % 2 == 0:
            lam_init = 0.8 - 0.6 * math.exp(-0.3 * layer)
            h = h + fox_diff_mixer(hn, even_w_in[i], even_b_forget[i], even_lambda_q1[i], even_lambda_k1[i],
                                   even_lambda_q2[i], even_lambda_k2[i], even_subln[i], even_w_out[i], lam_init)
        else:
            h = h + nsa_mixer(hn, odd_w_in[i], odd_cmp_pe_k[i], odd_cmp_k_w1[i], odd_cmp_k_b1[i], odd_cmp_k_w2[i],
                              odd_cmp_pe_v[i], odd_cmp_v_w1[i], odd_cmp_v_b1[i], odd_cmp_v_w2[i], odd_w_out[i])
        h = h + 0.5 * swiglu(rms_norm(h, ffn2_norm[layer]), ffn2_w_gate[layer], ffn2_w_up[layer], ffn2_w_down[layer])
    return rms_norm(h, final_norm)
```

```python
import functools
import math

import jax
import jax.numpy as jnp
from jax import lax
from jax.experimental import pallas as pl
from jax.experimental.pallas import tpu as pltpu

F32 = jnp.float32
MXU_DTYPE = jnp.bfloat16

HEAD_DIM = 128
ROPE_THETA = 10000.0
RMS_EPS = 1e-6
NEG = -1e30
BIG = 1e9
FOX_HEADS = 16
DIFF_HEADS = 8
DIFF_V_DIM = 2 * HEAD_DIM
NSA_HEADS = 32
NSA_KV_HEADS = 4
NSA_GROUP = NSA_HEADS // NSA_KV_HEADS
CMP_BLOCK = 32
CMP_STRIDE = 16
SEL_BLOCK = 64
N_SELECT = 16
WINDOW = 512
N_BRANCH = 3
LANES = 128

VMEM_LIMIT = 56 * 1024 * 1024


def _params(n_grid_dims):
    return pltpu.CompilerParams(dimension_semantics=("parallel",) * n_grid_dims,
                                vmem_limit_bytes=VMEM_LIMIT)


def _dot(a, b):
    return jnp.dot(a, b, preferred_element_type=F32)


def _dot_nt(a, b):
    return lax.dot_general(a, b, (((1,), (1,)), ((), ())), preferred_element_type=F32)


def _rope_rows(x, cos2, sin2):
    return x * cos2 + pltpu.roll(x, HEAD_DIM // 2, 1) * sin2


def _rms_kernel(x_ref, g_ref, o_ref):
    x = x_ref[...].astype(F32)
    ms = jnp.mean(x * x, axis=-1, keepdims=True)
    o_ref[...] = (x * lax.rsqrt(ms + RMS_EPS) * g_ref[...]).astype(o_ref.dtype)


def rms_norm(x, g, out_dtype, tm=256):
    m, d = x.shape
    tm = min(tm, m)
    return pl.pallas_call(
        _rms_kernel,
        out_shape=jax.ShapeDtypeStruct((m, d), out_dtype),
        grid=(m // tm,),
        in_specs=[pl.BlockSpec((tm, d), lambda i: (i, 0)),
                  pl.BlockSpec((1, d), lambda i: (0, 0))],
        out_specs=pl.BlockSpec((tm, d), lambda i: (i, 0)),
        compiler_params=_params(1),
        name="rms_norm",
    )(x, g.reshape(1, d).astype(F32))


def _ffn_up_kernel(x_ref, wg_ref, wu_ref, o_ref):
    x = x_ref[...]
    g = _dot(x, wg_ref[...])
    u = _dot(x, wu_ref[...])
    o_ref[...] = (g * jax.nn.sigmoid(g) * u).astype(o_ref.dtype)


def ffn_up(x, wg, wu, tm=1024, tn=256):
    m, k = x.shape
    n = wg.shape[1]
    tm, tn = min(tm, m), min(tn, n)
    return pl.pallas_call(
        _ffn_up_kernel,
        out_shape=jax.ShapeDtypeStruct((m, n), MXU_DTYPE),
        grid=(m // tm, n // tn),
        in_specs=[pl.BlockSpec((tm, k), lambda i, j: (i, 0)),
                  pl.BlockSpec((k, tn), lambda i, j: (0, j)),
                  pl.BlockSpec((k, tn), lambda i, j: (0, j))],
        out_specs=pl.BlockSpec((tm, tn), lambda i, j: (i, j)),
        compiler_params=_params(2),
        name="ffn_up",
    )(x, wg, wu)


def _mm_resid_kernel(a_ref, w_ref, r_ref, o_ref, *, scale):
    o_ref[...] = r_ref[...] + scale * _dot(a_ref[...], w_ref[...])


def mm_resid(a, w, resid, scale, tm, tn):
    m, k = a.shape
    n = w.shape[1]
    tm, tn = min(tm, m), min(tn, n)
    return pl.pallas_call(
        functools.partial(_mm_resid_kernel, scale=scale),
        out_shape=jax.ShapeDtypeStruct((m, n), F32),
        grid=(m // tm, n // tn),
        in_specs=[pl.BlockSpec((tm, k), lambda i, j: (i, 0)),
                  pl.BlockSpec((k, tn), lambda i, j: (0, j)),
                  pl.BlockSpec((tm, tn), lambda i, j: (i, j))],
        out_specs=pl.BlockSpec((tm, tn), lambda i, j: (i, j)),
        compiler_params=_params(2),
        name="mm_resid",
    )(a, w, resid)


def _mm_kernel(a_ref, w_ref, o_ref):
    o_ref[...] = _dot(a_ref[...], w_ref[...]).astype(o_ref.dtype)


def mm(a, w, out_dtype, tm, tn):
    m, k = a.shape
    n = w.shape[1]
    tm, tn = min(tm, m), min(tn, n)
    return pl.pallas_call(
        _mm_kernel,
        out_shape=jax.ShapeDtypeStruct((m, n), out_dtype),
        grid=(m // tm, n // tn),
        in_specs=[pl.BlockSpec((tm, k), lambda i, j: (i, 0)),
                  pl.BlockSpec((k, tn), lambda i, j: (0, j))],
        out_specs=pl.BlockSpec((tm, tn), lambda i, j: (i, j)),
        compiler_params=_params(2),
        name="mm",
    )(a, w)


def _mm_rope_kernel(a_ref, w_ref, cos_ref, sin_ref, o_ref, *, rope_tiles, tn):
    acc = _dot(a_ref[...], w_ref[...])
    j = pl.program_id(1)
    is_rope = functools.reduce(jnp.logical_or, [j == t for t in rope_tiles])

    @pl.when(is_rope)
    def _():
        cos2 = cos_ref[...]
        sin2 = sin_ref[...]
        for h in range(tn // HEAD_DIM):
            cols = slice(h * HEAD_DIM, (h + 1) * HEAD_DIM)
            o_ref[:, cols] = _rope_rows(acc[:, cols], cos2, sin2).astype(o_ref.dtype)

    @pl.when(jnp.logical_not(is_rope))
    def _():
        o_ref[...] = acc.astype(o_ref.dtype)


def mm_rope(a, w, cos2, sin2, rope_tiles, out_dtype, tm, tn):
    m, k = a.shape
    n = w.shape[1]
    t = cos2.shape[0]
    tm, tn = min(tm, t), min(tn, n)
    t_tiles = t // tm
    return pl.pallas_call(
        functools.partial(_mm_rope_kernel, rope_tiles=tuple(rope_tiles), tn=tn),
        out_shape=jax.ShapeDtypeStruct((m, n), out_dtype),
        grid=(m // tm, n // tn),
        in_specs=[pl.BlockSpec((tm, k), lambda i, j: (i, 0)),
                  pl.BlockSpec((k, tn), lambda i, j: (0, j)),
                  pl.BlockSpec((tm, HEAD_DIM), lambda i, j: (i % t_tiles, 0)),
                  pl.BlockSpec((tm, HEAD_DIM), lambda i, j: (i % t_tiles, 0))],
        out_specs=pl.BlockSpec((tm, tn), lambda i, j: (i, j)),
        compiler_params=_params(2),
        name="mm_rope",
    )(a, w, cos2, sin2)


def _forget_cum_kernel(f_ref, b_ref, o_ref):
    x = f_ref[...] + b_ref[...]
    x = jnp.minimum(x, 0.0) - jnp.log(1.0 + jnp.exp(-jnp.abs(x)))
    t = x.shape[0]
    row = lax.broadcasted_iota(jnp.int32, x.shape, 0)
    shift = 1
    while shift < t:
        x = x + jnp.where(row >= shift, pltpu.roll(x, shift, 0), 0.0)
        shift *= 2
    o_ref[...] = x


def forget_cum(fgate, bias, batch):
    m, n = fgate.shape
    t = m // batch
    return pl.pallas_call(
        _forget_cum_kernel,
        out_shape=jax.ShapeDtypeStruct((m, n), F32),
        grid=(batch,),
        in_specs=[pl.BlockSpec((t, n), lambda b: (b, 0)),
                  pl.BlockSpec((1, n), lambda b: (0, 0))],
        out_specs=pl.BlockSpec((t, n), lambda b: (b, 0)),
        compiler_params=_params(1),
        name="forget_cum",
    )(fgate, bias)


def _causal_flash(q, k_ref, k_cols, v_ref, i, tq, tk, scale, bias_fn=None):
    dv = v_ref.shape[1]
    tpos = i * tq + lax.broadcasted_iota(jnp.int32, (tq, tk), 0)
    n_chunks = ((i + 1) * tq + tk - 1) // tk

    def body(c, carry):
        m_prev, l_prev, acc = carry
        start = pl.multiple_of(c * tk, tk)
        k = k_ref[pl.ds(start, tk), k_cols]
        v = v_ref[pl.ds(start, tk), :]
        s = _dot_nt(q, k) * scale
        if bias_fn is not None:
            s = s + bias_fn(start)
        mask = tpos >= start + lax.broadcasted_iota(jnp.int32, (tq, tk), 1)
        s = jnp.where(mask, s, NEG)
        m_new = jnp.maximum(m_prev, jnp.max(s, axis=-1, keepdims=True))
        alpha = jnp.exp(m_prev - m_new)
        p = jnp.where(mask, jnp.exp(s - m_new), 0.0)
        l_new = alpha * l_prev + jnp.sum(p, axis=-1, keepdims=True)
        acc = alpha * acc + _dot(p.astype(v.dtype), v)
        return m_new, l_new, acc

    init = (jnp.full((tq, 1), NEG, F32), jnp.zeros((tq, 1), F32), jnp.zeros((tq, dv), F32))
    _, l, acc = lax.fori_loop(0, n_chunks, body, init)
    return acc / l


def _fox_kernel(q_ref, k_ref, v_ref, cq_ref, ck_ref, o_ref, *, tq, tk, scale):
    i = pl.program_id(2)
    cq = cq_ref[...]

    def bias(start):
        return cq - ck_ref[:, pl.ds(start, tk)]

    o = _causal_flash(q_ref[...], k_ref, slice(None), v_ref, i, tq, tk, scale, bias)
    o_ref[...] = o.astype(o_ref.dtype)


def fox_attention(proj, cum_col, cum_row, batch, t, tq=256, tk=256):
    tq, tk = min(tq, t), min(tk, t)
    nq = t // tq
    h = FOX_HEADS
    return pl.pallas_call(
        functools.partial(_fox_kernel, tq=tq, tk=tk, scale=HEAD_DIM ** -0.5),
        out_shape=jax.ShapeDtypeStruct((batch * t, h * HEAD_DIM), MXU_DTYPE),
        grid=(batch, h, nq),
        in_specs=[pl.BlockSpec((tq, HEAD_DIM), lambda b, hh, i: (b * nq + i, hh)),
                  pl.BlockSpec((t, HEAD_DIM), lambda b, hh, i: (b, h + hh)),
                  pl.BlockSpec((t, HEAD_DIM), lambda b, hh, i: (b, 2 * h + hh)),
                  pl.BlockSpec((None, None, tq, 1), lambda b, hh, i: (b, hh, i, 0)),
                  pl.BlockSpec((None, None, 1, t), lambda b, hh, i: (b, hh, 0, 0))],
        out_specs=pl.BlockSpec((tq, HEAD_DIM), lambda b, hh, i: (b * nq + i, hh)),
        compiler_params=_params(3),
        name="fox_attention",
    )(proj, proj, proj, cum_col, cum_row)


def _diff_kernel(q_ref, k_ref, v_ref, lam_ref, g_ref, o_ref, *, tq, tk, scale, lam_init):
    i = pl.program_id(2)
    lv = lam_ref[...]
    lam = (jnp.exp(jnp.sum(lv[0:1] * lv[1:2], axis=-1, keepdims=True))
           - jnp.exp(jnp.sum(lv[2:3] * lv[3:4], axis=-1, keepdims=True)) + lam_init)
    q = q_ref[...]
    o1 = _causal_flash(q[:, :HEAD_DIM], k_ref, slice(0, HEAD_DIM), v_ref, i, tq, tk, scale)
    o2 = _causal_flash(q[:, HEAD_DIM:], k_ref, slice(HEAD_DIM, 2 * HEAD_DIM), v_ref, i, tq, tk, scale)
    o = o1 - lam * o2
    ms = jnp.mean(o * o, axis=-1, keepdims=True)
    y = o * lax.rsqrt(ms + RMS_EPS) * g_ref[...]
    o_ref[...] = (y * (1.0 - lam_init)).astype(o_ref.dtype)


def diff_attention(proj, lam_vecs, subln_g, lam_init, batch, t, tq=256, tk=256):
    tq, tk = min(tq, t), min(tk, t)
    nq = t // tq
    h = DIFF_HEADS
    w = DIFF_V_DIM
    q0 = 3 * FOX_HEADS * HEAD_DIM // w
    return pl.pallas_call(
        functools.partial(_diff_kernel, tq=tq, tk=tk, scale=HEAD_DIM ** -0.5, lam_init=lam_init),
        out_shape=jax.ShapeDtypeStruct((batch * t, h * w), MXU_DTYPE),
        grid=(batch, h, nq),
        in_specs=[pl.BlockSpec((tq, w), lambda b, hh, i: (b * nq + i, q0 + hh)),
                  pl.BlockSpec((t, w), lambda b, hh, i: (b, q0 + h + hh)),
                  pl.BlockSpec((t, w), lambda b, hh, i: (b, q0 + 2 * h + hh)),
                  pl.BlockSpec((4, HEAD_DIM), lambda b, hh, i: (0, 0)),
                  pl.BlockSpec((1, w), lambda b, hh, i: (0, 0))],
        out_specs=pl.BlockSpec((tq, w), lambda b, hh, i: (b * nq + i, hh)),
        compiler_params=_params(3),
        name="diff_attention",
    )(proj, proj, proj, lam_vecs, subln_g)


def _compress_one(c_ref, pe_ref, w1_ref, b1_ref, w2_ref, o_ref):
    c = c_ref[...]
    n_chunks, half = c.shape
    u = _dot(c, w1_ref[:half, :])
    v = _dot(c, w1_ref[half:, :])
    pe = jnp.broadcast_to(pe_ref[...], (8, 2 * half))
    const = _dot(pe, w1_ref[...])[0:1] + b1_ref[...]
    hid = u + pltpu.roll(v, n_chunks - 1, 0) + const
    hid = hid * jax.nn.sigmoid(hid)
    o_ref[...] = _dot(hid.astype(w2_ref.dtype), w2_ref[...]).astype(o_ref.dtype)


def _compress_kernel(ck_ref, cv_ref, pek_ref, kw1_ref, kb1_ref, kw2_ref,
                     pev_ref, vw1_ref, vb1_ref, vw2_ref, ok_ref, ov_ref):
    _compress_one(ck_ref, pek_ref, kw1_ref, kb1_ref, kw2_ref, ok_ref)
    _compress_one(cv_ref, pev_ref, vw1_ref, vb1_ref, vw2_ref, ov_ref)


def nsa_compress(ck, cv, k_params, v_params):
    batch, groups, n_chunks, width = ck.shape
    act_spec = pl.BlockSpec((None, None, n_chunks, width), lambda b, g: (b, g, 0, 0))
    out_spec = pl.BlockSpec((None, None, n_chunks, HEAD_DIM), lambda b, g: (b, g, 0, 0))

    def full(a):
        return pl.BlockSpec(a.shape, lambda b, g: (0,) * a.ndim)

    out_sds = jax.ShapeDtypeStruct((batch, groups, n_chunks, HEAD_DIM), MXU_DTYPE)
    return pl.pallas_call(
        _compress_kernel,
        out_shape=(out_sds, out_sds),
        grid=(batch, groups),
        in_specs=[act_spec, act_spec] + [full(a) for a in k_params] + [full(a) for a in v_params],
        out_specs=(out_spec, out_spec),
        compiler_params=_params(2),
        name="nsa_compress",
    )(ck, cv, *k_params, *v_params)


def _nsa_kernel(q_ref, cos_ref, sin_ref, kc_ref, vc_ref, ks_ref, vs_ref, kw_ref, vw_ref, gate_ref,
                o_ref, m_sc, l_sc, acc_sc, *, tq, tk, t_len, scale):
    i = pl.program_id(2)
    r_heads = NSA_GROUP
    rows = r_heads * tq
    t0 = i * tq
    n_cmp = kc_ref.shape[0]
    n_sel = t_len // SEL_BLOCK
    n_top = min(N_SELECT, n_sel)

    q_all = q_ref[...]
    q_st = jnp.concatenate([q_all[:, r * HEAD_DIM:(r + 1) * HEAD_DIM] for r in range(r_heads)], axis=0)

    s = (_dot_nt(q_st, kc_ref[...]) * scale).reshape(r_heads, tq, n_cmp)
    tpos_c = t0 + lax.broadcasted_iota(jnp.int32, (tq, n_cmp), 0)
    blk_end = lax.broadcasted_iota(jnp.int32, (tq, n_cmp), 1) * CMP_STRIDE + (CMP_BLOCK - 1)
    cmask = (tpos_c >= blk_end)[None]
    s = jnp.where(cmask, s, NEG)
    e = jnp.exp(s - jnp.max(s, axis=-1, keepdims=True))
    p_cmp = jnp.where(cmask, e / jnp.sum(e, axis=-1, keepdims=True), 0.0)
    o_cmp = _dot(p_cmp.reshape(rows, n_cmp).astype(vc_ref.dtype), vc_ref[...])

    p_sum = jnp.sum(p_cmp, axis=0)
    cstart = lax.broadcasted_iota(jnp.int32, (n_cmp, n_sel), 0) * CMP_STRIDE
    sstart = lax.broadcasted_iota(jnp.int32, (n_cmp, n_sel), 1) * SEL_BLOCK
    overlap = jnp.where((cstart < sstart + SEL_BLOCK) & (cstart + CMP_BLOCK > sstart), 1.0, 0.0)
    imp = jnp.dot(p_sum, overlap, preferred_element_type=F32, precision=lax.Precision.HIGHEST)

    jidx = lax.broadcasted_iota(jnp.int32, (tq, n_sel), 1)
    blk_t = (t0 + lax.broadcasted_iota(jnp.int32, (tq, n_sel), 0)) // SEL_BLOCK
    valid = jidx <= blk_t
    forced = (jidx == 0) | (jidx == blk_t) | (jidx == blk_t - 1)
    score = jnp.where(valid, jnp.where(forced, BIG, imp), -BIG)
    rank = jnp.zeros((tq, n_sel), F32)
    for jp in range(n_sel):
        col = score[:, jp:jp + 1]
        ahead = (col > score) | ((col == score) & (jidx > jp))
        rank = rank + jnp.where(ahead, 1.0, 0.0)
    sel = jnp.where(valid & (rank < n_top), 1.0, 0.0).astype(MXU_DTYPE)

    cos2 = jnp.concatenate([cos_ref[...]] * r_heads, axis=0)
    sin2 = jnp.concatenate([sin_ref[...]] * r_heads, axis=0)
    q_rot = _rope_rows(q_st.astype(F32), cos2, sin2).astype(q_st.dtype)

    m_sc[...] = jnp.full(m_sc.shape, NEG, F32)
    l_sc[...] = jnp.zeros(l_sc.shape, F32)
    acc_sc[...] = jnp.zeros(acc_sc.shape, F32)
    tpos_k = t0 + lax.broadcasted_iota(jnp.int32, (tq, tk), 0)
    n_chunks = ((i + 1) * tq + tk - 1) // tk

    def sel_body(c, carry):
        start = pl.multiple_of(c * tk, tk)
        k = ks_ref[pl.ds(start, tk), :]
        v = vs_ref[pl.ds(start, tk), :]
        key_blk = (start + lax.broadcasted_iota(jnp.int32, (n_sel, tk), 1)) // SEL_BLOCK
        expand = jnp.where(key_blk == lax.broadcasted_iota(jnp.int32, (n_sel, tk), 0), 1.0, 0.0)
        chosen = _dot(sel, expand.astype(MXU_DTYPE))
        spos = start + lax.broadcasted_iota(jnp.int32, (tq, tk), 1)
        mask = ((chosen > 0.5) & (spos <= tpos_k))[None]
        sc = (_dot_nt(q_rot, k) * scale).reshape(r_heads, tq, tk)
        sc = jnp.where(mask, sc, NEG)
        m_prev = m_sc[...]
        m_new = jnp.maximum(m_prev, jnp.max(sc, axis=-1, keepdims=True))
        alpha = jnp.exp(m_prev - m_new)
        p = jnp.where(mask, jnp.exp(sc - m_new), 0.0)
        l_sc[...] = alpha * l_sc[...] + jnp.sum(p, axis=-1, keepdims=True)
        pv = _dot(p.reshape(rows, tk).astype(v.dtype), v)
        acc_sc[...] = alpha.reshape(rows, 1) * acc_sc[...] + pv
        m_sc[...] = m_new
        return carry

    lax.fori_loop(0, n_chunks, sel_body, 0)
    o_sel = acc_sc[...] / l_sc[...].reshape(rows, 1)

    span = min(WINDOW + tq, t_len)
    wstart = pl.multiple_of(jnp.maximum(t0 + tq - span, 0), tq)
    kwin = kw_ref[pl.ds(wstart, span), :]
    vwin = vw_ref[pl.ds(wstart, span), :]
    dist = (t0 + lax.broadcasted_iota(jnp.int32, (tq, span), 0)
            - (wstart + lax.broadcasted_iota(jnp.int32, (tq, span), 1)))
    wmask = ((dist >= 0) & (dist < WINDOW))[None]
    sw = (_dot_nt(q_rot, kwin) * scale).reshape(r_heads, tq, span)
    sw = jnp.where(wmask, sw, NEG)
    ew = jnp.where(wmask, jnp.exp(sw - jnp.max(sw, axis=-1, keepdims=True)), 0.0)
    pw = ew / jnp.sum(ew, axis=-1, keepdims=True)
    o_win = _dot(pw.reshape(rows, span).astype(vwin.dtype), vwin)

    gates = jax.nn.sigmoid(gate_ref[...])
    for r in range(r_heads):
        rs = slice(r * tq, (r + 1) * tq)
        out = (gates[:, 3 * r:3 * r + 1] * o_cmp[rs]
               + gates[:, 3 * r + 1:3 * r + 2] * o_sel[rs]
               + gates[:, 3 * r + 2:3 * r + 3] * o_win[rs])
        o_ref[:, r * HEAD_DIM:(r + 1) * HEAD_DIM] = out.astype(o_ref.dtype)


def nsa_attention(proj, gates, k_cmp, v_cmp, cos2, sin2, batch, t, tq=128, tk=256):
    tq, tk = min(tq, t), min(tk, t)
    nq = t // tq
    g_heads, r_heads = NSA_KV_HEADS, NSA_GROUP
    qw = r_heads * HEAD_DIM
    kv0 = g_heads * r_heads
    n_cmp = k_cmp.shape[2]
    rows = r_heads * tq

    def kv_spec(slot):
        return pl.BlockSpec((t, HEAD_DIM), lambda b, g, i: (b, kv0 + slot * g_heads + g))

    cmp_spec = pl.BlockSpec((None, None, n_cmp, HEAD_DIM), lambda b, g, i: (b, g, 0, 0))
    return pl.pallas_call(
        functools.partial(_nsa_kernel, tq=tq, tk=tk, t_len=t, scale=HEAD_DIM ** -0.5),
        out_shape=jax.ShapeDtypeStruct((batch * t, g_heads * qw), MXU_DTYPE),
        grid=(batch, g_heads, nq),
        in_specs=[pl.BlockSpec((tq, qw), lambda b, g, i: (b * nq + i, g)),
                  pl.BlockSpec((tq, HEAD_DIM), lambda b, g, i: (i, 0)),
                  pl.BlockSpec((tq, HEAD_DIM), lambda b, g, i: (i, 0)),
                  cmp_spec, cmp_spec,
                  kv_spec(2), kv_spec(3), kv_spec(4), kv_spec(5),
                  pl.BlockSpec((tq, LANES), lambda b, g, i: (b * nq + i, g))],
        out_specs=pl.BlockSpec((tq, qw), lambda b, g, i: (b * nq + i, g)),
        scratch_shapes=[pltpu.VMEM((r_heads, tq, 1), F32),
                        pltpu.VMEM((r_heads, tq, 1), F32),
                        pltpu.VMEM((rows, HEAD_DIM), F32)],
        compiler_params=_params(3),
        name="nsa_attention",
    )(proj, cos2, sin2, k_cmp, v_cmp, proj, proj, proj, proj, gates)


def _rope_tables(t):
    half = HEAD_DIM // 2
    inv = ROPE_THETA ** (-jnp.arange(half, dtype=F32) / half)
    ang = jnp.arange(t, dtype=F32)[:, None] * inv[None, :]
    cos, sin = jnp.cos(ang), jnp.sin(ang)
    return jnp.concatenate([cos, cos], axis=-1), jnp.concatenate([-sin, sin], axis=-1)


def _pad_cols(w, width):
    return jnp.pad(w, ((0, 0), (0, width - w.shape[1])))


def swiglu_half_step(h, norm_g, w_gate, w_up, w_down):
    xn = rms_norm(h, norm_g, MXU_DTYPE)
    act = ffn_up(xn, w_gate.astype(MXU_DTYPE), w_up.astype(MXU_DTYPE))
    return mm_resid(act, w_down.astype(MXU_DTYPE), h, 0.5, tm=512, tn=512)


def fox_diff_mixer(h, hn, batch, t, w_in, b_forget, lq1, lk1, lq2, lk2, subln_g, w_out, lam_init, cos2, sin2):
    n_fox = 3 * FOX_HEADS * HEAD_DIM
    w_main = jnp.concatenate([w_in[:, :n_fox], w_in[:, n_fox + FOX_HEADS:]], axis=1).astype(MXU_DTYPE)
    w_gate = _pad_cols(w_in[:, n_fox:n_fox + FOX_HEADS], LANES).astype(MXU_DTYPE)
    tn = 1024
    d_w = DIFF_HEADS * 2 * HEAD_DIM
    rope_tiles = range(n_fox // tn, (n_fox + 2 * d_w) // tn)
    proj = mm_rope(hn, w_main, cos2, sin2, rope_tiles, MXU_DTYPE, tm=1024, tn=tn)
    fgate = mm(hn, w_gate, F32, tm=1024, tn=LANES)
    bias = _pad_cols(b_forget.reshape(1, FOX_HEADS).astype(F32), LANES)
    cum = forget_cum(fgate, bias, batch)[:, :FOX_HEADS].reshape(batch, t, FOX_HEADS)
    cum = jnp.transpose(cum, (0, 2, 1))
    o_fox = fox_attention(proj, cum[..., None], cum[:, :, None, :], batch, t)
    lam_vecs = jnp.stack([lq1, lk1, lq2, lk2]).astype(F32)
    o_diff = diff_attention(proj, lam_vecs, subln_g.reshape(1, DIFF_V_DIM).astype(F32), lam_init, batch, t)
    mixed = jnp.concatenate([o_fox, o_diff], axis=1)
    return mm_resid(mixed, w_out.astype(MXU_DTYPE), h, 1.0, tm=1024, tn=512)


def nsa_mixer(h, hn, batch, t, w_in, pe_k, k_w1, k_b1, k_w2, pe_v, v_w1, v_b1, v_w2, w_out, cos2, sin2):
    g_heads, r_heads, d = NSA_KV_HEADS, NSA_GROUP, HEAD_DIM
    n_q = NSA_HEADS * d
    n_main = n_q + 6 * g_heads * d
    w_main = w_in[:, :n_main].astype(MXU_DTYPE)
    w_gates = w_in[:, n_main:].reshape(-1, g_heads, r_heads * N_BRANCH)
    w_gates = jnp.pad(w_gates, ((0, 0), (0, 0), (0, LANES - r_heads * N_BRANCH)))
    w_gates = w_gates.reshape(-1, g_heads * LANES).astype(MXU_DTYPE)
    tn = g_heads * d
    rope_tiles = (n_q // tn + 2, n_q // tn + 4)
    proj = mm_rope(hn, w_main, cos2, sin2, rope_tiles, MXU_DTYPE, tm=1024, tn=tn)
    gates = mm(hn, w_gates, F32, tm=1024, tn=g_heads * LANES)

    def chunks(col0):
        a = proj[:, col0:col0 + g_heads * d].reshape(batch, t // CMP_STRIDE, CMP_STRIDE, g_heads, d)
        return jnp.transpose(a, (0, 3, 1, 2, 4)).reshape(batch, g_heads, t // CMP_STRIDE, CMP_STRIDE * d)

    def cmp_params(pe, w1, b1, w2):
        return (pe.reshape(1, CMP_BLOCK * d).astype(MXU_DTYPE), w1.astype(MXU_DTYPE),
                b1.reshape(1, -1).astype(F32), w2.astype(MXU_DTYPE))

    k_cmp, v_cmp = nsa_compress(chunks(n_q), chunks(n_q + g_heads * d),
                                cmp_params(pe_k, k_w1, k_b1, k_w2), cmp_params(pe_v, v_w1, v_b1, v_w2))
    mixed = nsa_attention(proj, gates, k_cmp, v_cmp, cos2, sin2, batch, t)
    return mm_resid(mixed, w_out.astype(MXU_DTYPE), h, 1.0, tm=1024, tn=512)


def kernel(x, ffn1_norm, ffn1_w_gate, ffn1_w_up, ffn1_w_down, mix_norm, ffn2_norm, ffn2_w_gate, ffn2_w_up, ffn2_w_down, even_w_in, even_b_forget, even_lambda_q1, even_lambda_k1, even_lambda_q2, even_lambda_k2, even_subln, even_w_out, odd_w_in, odd_cmp_pe_k, odd_cmp_k_w1, odd_cmp_k_b1, odd_cmp_k_w2, odd_cmp_pe_v, odd_cmp_v_w1, odd_cmp_v_b1, odd_cmp_v_w2, odd_w_out, final_norm):
    batch, t, d_model = x.shape
    depth = ffn1_norm.shape[0]
    cos2, sin2 = _rope_tables(t)
    h = x.reshape(batch * t, d_model).astype(F32)
    for layer in range(depth):
        h = swiglu_half_step(h, ffn1_norm[layer], ffn1_w_gate[layer], ffn1_w_up[layer], ffn1_w_down[layer])
        hn = rms_norm(h, mix_norm[layer], MXU_DTYPE)
        i = layer // 2
        if layer % 2 == 0:
            lam_init = 0.8 - 0.6 * math.exp(-0.3 * layer)
            h = fox_diff_mixer(h, hn, batch, t, even_w_in[i], even_b_forget[i], even_lambda_q1[i],
                               even_lambda_k1[i], even_lambda_q2[i], even_lambda_k2[i], even_subln[i],
                               even_w_out[i], lam_init, cos2, sin2)
        else:
            h = nsa_mixer(h, hn, batch, t, odd_w_in[i], odd_cmp_pe_k[i], odd_cmp_k_w1[i], odd_cmp_k_b1[i],
                          odd_cmp_k_w2[i], odd_cmp_pe_v[i], odd_cmp_v_w1[i], odd_cmp_v_b1[i], odd_cmp_v_w2[i],
                          odd_w_out[i], cos2, sin2)
        h = swiglu_half_step(h, ffn2_norm[layer], ffn2_w_gate[layer], ffn2_w_up[layer], ffn2_w_down[layer])
    return rms_norm(h, final_norm, x.dtype).reshape(batch, t, d_model)
```
